```python
import jax, jax.numpy as jnp
from jax import lax
import numpy as np

D_MODEL = 1024
BATCH = 2
SEQ = 8192
DEPTH = 4
DEC_BATCH = 32
DEC_SEQ = 64
PAST_LEN = 1024

CHUNK = 64
HEAD_DIM = 64
H_FOX = 8
H_BAND = 8
W_FOX = H_FOX * HEAD_DIM
W_BAND = H_BAND * HEAD_DIM
Q_BLOCK = 128
LEFT_CHUNKS = 8
WINDOW_ROWS = LEFT_CHUNKS * CHUNK
REL_CLIP = 128
H_GDN = 8
DK_GDN = 128
DV_GDN = 128
W_GDN = H_GDN * DV_GDN
CONV_CH = 2 * H_GDN * DK_GDN + H_GDN * DV_GDN
CONV_W = 4
GDN_CHUNK = 64
N_EXPERTS = 32
N_GROUPS = 8
EXPERTS_PER_GROUP = N_EXPERTS // N_GROUPS
TOP_K = 2
D_EXPERT = 512
EXPERT_BLOCK = 128
N_ATT = (DEPTH + 1) // 2
N_GDN = DEPTH // 2
ALPHA = (2 * DEPTH) ** 0.25
BETA = (8 * DEPTH) ** -0.25
LN_EPS = 1e-5
NORM_EPS = 1e-6
NEG_INF = -1e30

kernel_name = "hybrid_streaming_fox_band_gdn_moe_step"


def layer_norm(x, g, b):
    xf = x.astype(jnp.float32)
    mu = jnp.mean(xf, axis=-1, keepdims=True)
    var = jnp.mean(jnp.square(xf - mu), axis=-1, keepdims=True)
    return ((xf - mu) * lax.rsqrt(var + LN_EPS) * g + b).astype(x.dtype)


def heads(t, h):
    return t.reshape(t.shape[:-1] + (h, t.shape[-1] // h))


def masked_attention(q, k, v, bias, mask):
    s = jnp.einsum('bqhd,bkhd->bhqk', q, k, preferred_element_type=jnp.float32) * (q.shape[-1] ** -0.5)
    p = jax.nn.softmax(jnp.where(mask, s + bias, NEG_INF), axis=-1)
    return jnp.einsum('bhqk,bkhd->bqhd', p.astype(v.dtype), v)


def rel_bias(table, dist):
    idx = jnp.clip(dist, -REL_CLIP, REL_CLIP) + REL_CLIP
    return table[:, idx].astype(jnp.float32)


def split_att(proj):
    cuts = [W_FOX, 2 * W_FOX, 3 * W_FOX, 3 * W_FOX + H_FOX,
            3 * W_FOX + H_FOX + W_BAND, 3 * W_FOX + H_FOX + 2 * W_BAND]
    return jnp.split(proj, cuts, axis=-1)


def fox_prompt(q, k, v, logf):
    B, T, H, D = q.shape
    cT = jnp.cumsum(logf, axis=1).transpose(0, 2, 1)
    kpos = jnp.arange(T)

    def block(i):
        start = i * Q_BLOCK
        qb = lax.dynamic_slice_in_dim(q, start, Q_BLOCK, axis=1)
        cq = lax.dynamic_slice_in_dim(cT, start, Q_BLOCK, axis=2)
        bias = cq[..., :, None] - cT[:, :, None, :]
        mask = kpos[None, :] <= (start + jnp.arange(Q_BLOCK))[:, None]
        return masked_attention(qb, k, v, bias, mask)

    out = lax.map(block, jnp.arange(T // Q_BLOCK))
    return out.transpose(1, 0, 2, 3, 4).reshape(B, T, H, D)


def band_prompt(q, k, v, table):
    B, T, H, D = q.shape
    NC = T // CHUNK
    NBAND = (LEFT_CHUNKS + 1) * CHUNK

    def band(t):
        tp = jnp.pad(t.reshape(B, NC, CHUNK, H, D), ((0, 0), (LEFT_CHUNKS, 0), (0, 0), (0, 0), (0, 0)))
        kb = jnp.concatenate([tp[:, j:j + NC] for j in range(LEFT_CHUNKS + 1)], axis=2)
        return kb.reshape(B * NC, NBAND, H, D)

    krel = jnp.arange(NBAND) - WINDOW_ROWS
    bias = rel_bias(table, jnp.arange(CHUNK)[:, None] - krel[None, :])[None]
    kabs = jnp.arange(NC)[:, None] * CHUNK + krel[None, :]
    valid = jnp.tile(kabs >= 0, (B, 1))[:, None, None, :]
    out = masked_attention(q.reshape(B * NC, CHUNK, H, D), band(k), band(v), bias, valid)
    return out.reshape(B, T, H, D)


def att_prompt(x, w_in, b_f, table, w_out, band_rows):
    B, T, _ = x.shape
    qf, kf, vf, fl, qb, kb, vb = split_att(x @ w_in)
    qf, kf, vf = heads(qf, H_FOX), heads(kf, H_FOX), heads(vf, H_FOX)
    qb, kb, vb = heads(qb, H_BAND), heads(kb, H_BAND), heads(vb, H_BAND)
    logf = jax.nn.log_sigmoid(fl.astype(jnp.float32) + b_f)
    of = fox_prompt(qf, kf, vf, logf)
    ob = band_prompt(qb, kb, vb, table)
    y = jnp.concatenate([of.reshape(B, T, W_FOX), ob.reshape(B, T, W_BAND)], axis=-1) @ w_out
    pad = ((0, 0), (max(band_rows - T, 0), 0), (0, 0), (0, 0))
    buf_k = jnp.pad(kb, pad)[:, -band_rows:]
    buf_v = jnp.pad(vb, pad)[:, -band_rows:]
    return y, kf, vf, logf.astype(x.dtype), buf_k, buf_v


def att_sample(x, ck, cv, clogf, cbk, cbv, w_in, b_f, table, w_out):
    B, S, _ = x.shape
    P = ck.shape[1]
    R = cbk.shape[1]
    qf, kf, vf, fl, qb, kb, vb = split_att(x @ w_in)
    qf, kf, vf = heads(qf, H_FOX), heads(kf, H_FOX), heads(vf, H_FOX)
    qb, kb, vb = heads(qb, H_BAND), heads(kb, H_BAND), heads(vb, H_BAND)
    logf = jax.nn.log_sigmoid(fl.astype(jnp.float32) + b_f)
    k_all = jnp.concatenate([ck, kf], axis=1)
    v_all = jnp.concatenate([cv, vf], axis=1)
    cT = jnp.cumsum(jnp.concatenate([clogf.astype(jnp.float32), logf], axis=1), axis=1).transpose(0, 2, 1)
    bias = cT[:, :, P:, None] - cT[:, :, None, :]
    mask = jnp.arange(P + S)[None, :] <= (P + jnp.arange(S))[:, None]
    of = masked_attention(qf, k_all, v_all, bias, mask)
    kb_all = jnp.concatenate([cbk, kb], axis=1)
    vb_all = jnp.concatenate([cbv, vb], axis=1)
    krel = jnp.concatenate([jnp.arange(-R, 0), jnp.arange(S)])
    bias_b = rel_bias(table, jnp.arange(S)[:, None] - krel[None, :])[None]
    ob = masked_attention(qb, kb_all, vb_all, bias_b, jnp.ones((S, R + S), dtype=bool))
    y = jnp.concatenate([of.reshape(B, S, W_FOX), ob.reshape(B, S, W_BAND)], axis=-1) @ w_out
    return y, kf, vf, logf.astype(x.dtype), kb_all[:, -R:], vb_all[:, -R:]


def causal_conv(u, buf, w):
    T = u.shape[1]
    up = jnp.concatenate([buf, u], axis=1)
    y = up[:, 0:T] * w[0]
    for j in range(1, CONV_W):
        y = y + up[:, j:j + T] * w[j]
    return jax.nn.silu(y), up[:, -(CONV_W - 1):]


def l2norm(t):
    return t * lax.rsqrt(jnp.sum(jnp.square(t), axis=-1, keepdims=True) + NORM_EPS)


def gdn_chunked(q, k, v, g, beta, S0, L):
    B, T, H, DK = q.shape
    DV = v.shape[-1]
    NC = T // L

    def chunks(t):
        return t.reshape((B, NC, L) + t.shape[2:]).swapaxes(2, 3)

    qc, kc, vc, gc, bc = chunks(q), chunks(k), chunks(v), chunks(g), chunks(beta)
    gcum = jnp.cumsum(gc, axis=-1)
    diff = gcum[..., :, None] - gcum[..., None, :]
    ii = jnp.arange(L)
    strict = ii[:, None] > ii[None, :]
    incl = ii[:, None] >= ii[None, :]
    a = jnp.einsum('bnhid,bnhjd->bnhij', kc, kc) * jnp.exp(jnp.where(strict, diff, -jnp.inf)) * bc[..., :, None]
    rhs = jnp.concatenate([bc[..., None] * vc, (bc * jnp.exp(gcum))[..., None] * kc], axis=-1)
    sol = lax.linalg.triangular_solve(a, rhs, left_side=True, lower=True, unit_diagonal=True)
    u, w = sol[..., :DV], sol[..., DV:]
    qk = jnp.einsum('bnhid,bnhjd->bnhij', qc, kc) * jnp.exp(jnp.where(incl, diff, -jnp.inf))
    decay_q = jnp.exp(gcum)[..., None]
    glast = gcum[..., -1]
    k_end = kc * jnp.exp(glast[..., None] - gcum)[..., None]

    def step(S, xs):
        qn, un, wn, qkn, dqn, kn, gl = xs
        delta = un - jnp.einsum('bhld,bhde->bhle', wn, S)
        o = dqn * jnp.einsum('bhld,bhde->bhle', qn, S) + jnp.einsum('bhij,bhje->bhie', qkn, delta)
        S = jnp.exp(gl)[..., None, None] * S + jnp.einsum('bhld,bhle->bhde', kn, delta)
        return S, o

    xs = tuple(t.swapaxes(0, 1) for t in (qc, u, w, qk, decay_q, k_end, glast))
    S, o = lax.scan(step, S0, xs)
    return o.transpose(1, 0, 3, 2, 4).reshape(B, T, H, DV), S


def gdn_mixer(x, conv_buf, S0, w_in, conv_w, a_log, dt_bias, norm_w, w_out, L):
    B, T, _ = x.shape
    proj = x @ w_in
    qkv, z, a, b = jnp.split(proj, [CONV_CH, CONV_CH + W_GDN, CONV_CH + W_GDN + H_GDN], axis=-1)
    qkv, new_buf = causal_conv(qkv, conv_buf, conv_w)
    q, k, v = jnp.split(qkv.astype(jnp.float32), [H_GDN * DK_GDN, 2 * H_GDN * DK_GDN], axis=-1)
    q = l2norm(heads(q, H_GDN)) * (DK_GDN ** -0.5)
    k = l2norm(heads(k, H_GDN))
    v = heads(v, H_GDN)
    g = -jnp.exp(a_log.astype(jnp.float32)) * jax.nn.softplus(a.astype(jnp.float32) + dt_bias)
    beta = jax.nn.sigmoid(b.astype(jnp.float32))
    o, S = gdn_chunked(q, k, v, g, beta, S0.astype(jnp.float32), L)
    o = o * lax.rsqrt(jnp.mean(jnp.square(o), axis=-1, keepdims=True) + NORM_EPS) * norm_w
    o = o * jax.nn.silu(heads(z.astype(jnp.float32), H_GDN))
    y = o.reshape(B, T, W_GDN).astype(x.dtype) @ w_out
    return y, new_buf, S.astype(x.dtype)


def grouped_experts(xt, e_flat, g_flat, w_gate, w_up, w_down):
    N, D = xt.shape
    NK = e_flat.shape[0]
    order = jnp.argsort(e_flat)
    e_s = e_flat[order]
    tok_s = order // TOP_K
    g_s = g_flat[order]
    counts = jnp.bincount(e_flat, length=N_EXPERTS)
    padded = (counts + EXPERT_BLOCK - 1) // EXPERT_BLOCK * EXPERT_BLOCK
    pad_end = jnp.cumsum(padded)
    pad_start = pad_end - padded
    start = jnp.cumsum(counts) - counts
    dest = pad_start[e_s] + jnp.arange(NK) - start[e_s]
    n_blocks = -(-NK // EXPERT_BLOCK) + N_EXPERTS
    buf = jnp.zeros((n_blocks * EXPERT_BLOCK, D), xt.dtype).at[dest].set(xt[tok_s])
    block_expert = jnp.minimum(jnp.searchsorted(pad_end, jnp.arange(n_blocks) * EXPERT_BLOCK, side='right'), N_EXPERTS - 1)

    def run(args):
        xb, e = args
        return (jax.nn.silu(xb @ w_gate[e]) * (xb @ w_up[e])) @ w_down[e]

    out = lax.map(run, (buf.reshape(n_blocks, EXPERT_BLOCK, D), block_expert)).reshape(-1, D)
    return jax.ops.segment_sum(out[dest] * g_s[:, None].astype(out.dtype), tok_s, num_segments=N)


def moe(x, w_router, router_bias, w_gate, w_up, w_down):
    shape = x.shape
    xt = x.reshape(-1, shape[-1])
    N = xt.shape[0]
    scores = jax.nn.sigmoid(jnp.dot(xt, w_router, preferred_element_type=jnp.float32))
    sel = (scores + router_bias).reshape(N, N_GROUPS, EXPERTS_PER_GROUP)
    group_score = jnp.sum(lax.top_k(sel, 2)[0], axis=-1)
    best = jnp.argmax(group_score, axis=-1)
    in_group = jnp.take_along_axis(sel, best[:, None, None], axis=1)[:, 0]
    _, local = lax.top_k(in_group, TOP_K)
    expert = best[:, None] * EXPERTS_PER_GROUP + local
    gate = jnp.take_along_axis(scores, expert, axis=1)
    gate = gate / jnp.sum(gate, axis=-1, keepdims=True)
    y = grouped_experts(xt, expert.reshape(-1), gate.reshape(-1), w_gate, w_up, w_down)
    return y.reshape(shape)


def setup_inputs(seed: int = 0) -> dict:
    key = jax.random.key(seed)
    ks = jax.random.split(key, 32)

    def nrm(k, shape, scale):
        return jax.random.normal(k, shape, jnp.float32) * scale

    def uni(k, shape, lo, hi):
        return jax.random.uniform(k, shape, jnp.float32, minval=lo, maxval=hi)

    R = min(WINDOW_ROWS, PAST_LEN)
    d_in_att = 3 * W_FOX + H_FOX + 3 * W_BAND
    d_in_gdn = CONV_CH + W_GDN + 2 * H_GDN
    return {
        "x_prompt": nrm(ks[0], (BATCH, SEQ, D_MODEL), 1.0),
        "x_sample": nrm(ks[1], (DEC_BATCH, DEC_SEQ, D_MODEL), 1.0),
        "cache_fox_k": nrm(ks[2], (N_ATT, DEC_BATCH, PAST_LEN, H_FOX, HEAD_DIM), 1.0),
        "cache_fox_v": nrm(ks[3], (N_ATT, DEC_BATCH, PAST_LEN, H_FOX, HEAD_DIM), 1.0),
        "cache_fox_logf": jax.nn.log_sigmoid(uni(ks[4], (N_ATT, DEC_BATCH, PAST_LEN, H_FOX), 1.0, 4.0) + nrm(ks[5], (N_ATT, DEC_BATCH, PAST_LEN, H_FOX), 1.0)),
        "cache_band_k": nrm(ks[6], (N_ATT, DEC_BATCH, R, H_BAND, HEAD_DIM), 1.0),
        "cache_band_v": nrm(ks[7], (N_ATT, DEC_BATCH, R, H_BAND, HEAD_DIM), 1.0),
        "state_gdn": nrm(ks[8], (N_GDN, DEC_BATCH, H_GDN, DK_GDN, DV_GDN), 0.1),
        "state_gdn_conv": nrm(ks[9], (N_GDN, DEC_BATCH, CONV_W - 1, CONV_CH), 1.0),
        "w_in_att": nrm(ks[10], (N_ATT, D_MODEL, d_in_att), D_MODEL ** -0.5),
        "b_forget": uni(ks[11], (N_ATT, H_FOX), 1.0, 4.0),
        "rel_bias_table": nrm(ks[12], (N_ATT, H_BAND, 2 * REL_CLIP + 1), 0.2),
        "w_out_att": nrm(ks[13], (N_ATT, W_FOX + W_BAND, D_MODEL), BETA * (W_FOX + W_BAND) ** -0.5),
        "w_in_gdn": nrm(ks[14], (N_GDN, D_MODEL, d_in_gdn), D_MODEL ** -0.5),
        "conv_gdn": nrm(ks[15], (N_GDN, CONV_W, CONV_CH), CONV_W ** -0.5),
        "a_log": jnp.log(uni(ks[16], (N_GDN, H_GDN), 1.0, 16.0)),
        "dt_bias": jnp.log(jnp.expm1(uni(ks[17], (N_GDN, H_GDN), 0.001, 0.1))),
        "gdn_norm_w": 1.0 + nrm(ks[18], (N_GDN, DV_GDN), 0.02),
        "w_out_gdn": nrm(ks[19], (N_GDN, W_GDN, D_MODEL), BETA * W_GDN ** -0.5),
        "ln_mix_g": 1.0 + nrm(ks[20], (DEPTH, D_MODEL), 0.02),
        "ln_mix_b": nrm(ks[21], (DEPTH, D_MODEL), 0.02),
        "ln_ffn_g": 1.0 + nrm(ks[22], (DEPTH, D_MODEL), 0.02),
        "ln_ffn_b": nrm(ks[23], (DEPTH, D_MODEL), 0.02),
        "w_router": nrm(ks[24], (D_MODEL, N_EXPERTS), D_MODEL ** -0.5),
        "router_bias": nrm(ks[25], (N_EXPERTS,), 0.01),
        "w_gate": nrm(ks[26], (DEPTH, N_EXPERTS, D_MODEL, D_EXPERT), D_MODEL ** -0.5),
        "w_up": nrm(ks[27], (DEPTH, N_EXPERTS, D_MODEL, D_EXPERT), D_MODEL ** -0.5),
        "w_down": nrm(ks[28], (DEPTH, N_EXPERTS, D_EXPERT, D_MODEL), BETA * D_EXPERT ** -0.5),
    }


def reference(x_prompt, x_sample, cache_fox_k, cache_fox_v, cache_fox_logf, cache_band_k, cache_band_v,
              state_gdn, state_gdn_conv, w_in_att, b_forget, rel_bias_table, w_out_att, w_in_gdn, conv_gdn,
              a_log, dt_bias, gdn_norm_w, w_out_gdn, ln_mix_g, ln_mix_b, ln_ffn_g, ln_ffn_b,
              w_router, router_bias, w_gate, w_up, w_down):
    band_rows = cache_band_k.shape[2]
    xp, xs = x_prompt, x_sample
    fkp, fvp, flp, bkp, bvp, gsp, gcp = [], [], [], [], [], [], []
    fks, fvs, fls, bks, bvs, gss, gcs = [], [], [], [], [], [], []
    for layer in range(DEPTH):
        i = layer // 2
        if layer % 2 == 0:
            yp, k1, v1, l1, bk1, bv1 = att_prompt(xp, w_in_att[i], b_forget[i], rel_bias_table[i], w_out_att[i], band_rows)
            ys, k2, v2, l2, bk2, bv2 = att_sample(xs, cache_fox_k[i], cache_fox_v[i], cache_fox_logf[i],
                                                  cache_band_k[i], cache_band_v[i],
                                                  w_in_att[i], b_forget[i], rel_bias_table[i], w_out_att[i])
            fkp.append(k1); fvp.append(v1); flp.append(l1); bkp.append(bk1); bvp.append(bv1)
            fks.append(k2); fvs.append(v2); fls.append(l2); bks.append(bk2); bvs.append(bv2)
        else:
            B = xp.shape[0]
            yp, c1, s1 = gdn_mixer(xp, jnp.zeros((B, CONV_W - 1, CONV_CH), xp.dtype),
                                   jnp.zeros((B, H_GDN, DK_GDN, DV_GDN), jnp.float32),
                                   w_in_gdn[i], conv_gdn[i], a_log[i], dt_bias[i], gdn_norm_w[i], w_out_gdn[i], GDN_CHUNK)
            ys, c2, s2 = gdn_mixer(xs, state_gdn_conv[i], state_gdn[i],
                                   w_in_gdn[i], conv_gdn[i], a_log[i], dt_bias[i], gdn_norm_w[i], w_out_gdn[i], xs.shape[1])
            gcp.append(c1); gsp.append(s1); gcs.append(c2); gss.append(s2)
        xp = layer_norm(ALPHA * xp + yp, ln_mix_g[layer], ln_mix_b[layer])
        xs = layer_norm(ALPHA * xs + ys, ln_mix_g[layer], ln_mix_b[layer])
        xp = layer_norm(ALPHA * xp + moe(xp, w_router, router_bias, w_gate[layer], w_up[layer], w_down[layer]),
                        ln_ffn_g[layer], ln_ffn_b[layer])
        xs = layer_norm(ALPHA * xs + moe(xs, w_router, router_bias, w_gate[layer], w_up[layer], w_down[layer]),
                        ln_ffn_g[layer], ln_ffn_b[layer])
    return (xp, xs,
            jnp.stack(fkp), jnp.stack(fvp), jnp.stack(flp), jnp.stack(bkp), jnp.stack(bvp), jnp.stack(gsp), jnp.stack(gcp),
            jnp.stack(fks), jnp.stack(fvs), jnp.stack(fls), jnp.stack(bks), jnp.stack(bvs), jnp.stack(gss), jnp.stack(gcs))
```

```python
import functools

import numpy as np
import jax
import jax.numpy as jnp
from jax import lax
from jax.experimental import pallas as pl
from jax.experimental.pallas import tpu as pltpu

F32 = jnp.float32
BF16 = jnp.bfloat16
I32 = jnp.int32

CHUNK = 64
HEAD_DIM = 64
LEFT_CHUNKS = 8
WINDOW_ROWS = LEFT_CHUNKS * CHUNK
REL_CLIP = 128
N_EXPERTS = 32
N_GROUPS = 8
EXPERTS_PER_GROUP = N_EXPERTS // N_GROUPS
TOP_K = 2
CONV_W = 4
GDN_CHUNK = 64
LN_EPS = 1e-5
NORM_EPS = 1e-6
NEG_INF = -1e30

LANES = 128
SUBLANES = 8
VMEM_LIMIT = 56 * 1024 * 1024

HI = lax.Precision.HIGHEST


def _cparams(sem):
    return pltpu.CompilerParams(dimension_semantics=sem, vmem_limit_bytes=VMEM_LIMIT)


def _dot(a, b, **kw):
    return jnp.dot(a, b, preferred_element_type=F32, **kw)


def _dot_nt(a, b, **kw):
    return lax.dot_general(a, b, (((1,), (1,)), ((), ())), preferred_element_type=F32, **kw)


def _dot_tn(a, b, **kw):
    return lax.dot_general(a, b, (((0,), (0,)), ((), ())), preferred_element_type=F32, **kw)


def _mm(a, b):
    return _dot(a.astype(BF16), b.astype(BF16))


def _sigmoid(x):
    return 1.0 / (1.0 + jnp.exp(-x))


def _silu(x):
    return x * _sigmoid(x)


def _log_sigmoid(x):
    return jnp.minimum(x, 0.0) - jnp.log(1.0 + jnp.exp(-jnp.abs(x)))


def _softplus(x):
    return jnp.maximum(x, 0.0) + jnp.log(1.0 + jnp.exp(-jnp.abs(x)))


def _proj_kernel(x_ref, w_ref, *out_refs, segs):
    x = x_ref[...].astype(BF16)
    for o_ref, (start, width) in zip(out_refs, segs):
        o_ref[...] = _dot(x, w_ref[:, start:start + width]).astype(o_ref.dtype)


def proj(x, w, segs, dtypes, tm):
    M, K = x.shape
    out_shape = [jax.ShapeDtypeStruct((M, wd), dt) for (_, wd), dt in zip(segs, dtypes)]
    return pl.pallas_call(
        functools.partial(_proj_kernel, segs=tuple(segs)),
        grid=(M // tm,),
        in_specs=[pl.BlockSpec((tm, K), lambda i: (i, 0)),
                  pl.BlockSpec(w.shape, lambda i: (0, 0))],
        out_specs=[pl.BlockSpec((tm, wd), lambda i: (i, 0)) for (_, wd) in segs],
        out_shape=out_shape,
        compiler_params=_cparams(("parallel",)),
        name="proj",
    )(x, w)


def _layer_norm(h, g, b):
    mu = jnp.mean(h, axis=-1, keepdims=True)
    d = h - mu
    var = jnp.mean(d * d, axis=-1, keepdims=True)
    return d * lax.rsqrt(var + LN_EPS) * g + b


def _out_ln_kernel(*refs, n_in, alpha):
    m_refs = refs[:n_in]
    w_refs = refs[n_in:2 * n_in]
    x_ref, g_ref, b_ref, o_ref = refs[2 * n_in:]
    y = None
    for m_ref, w_ref in zip(m_refs, w_refs):
        t = _dot(m_ref[...].astype(BF16), w_ref[...])
        y = t if y is None else y + t
    o_ref[...] = _layer_norm(alpha * x_ref[...] + y, g_ref[...], b_ref[...])


def out_ln(ms, ws, x, g, b, alpha, tm):
    M, D = x.shape
    n_in = len(ms)
    in_specs = ([pl.BlockSpec((tm, m.shape[1]), lambda i: (i, 0)) for m in ms]
                + [pl.BlockSpec(w.shape, lambda i: (0, 0)) for w in ws]
                + [pl.BlockSpec((tm, D), lambda i: (i, 0)),
                   pl.BlockSpec((1, D), lambda i: (0, 0)),
                   pl.BlockSpec((1, D), lambda i: (0, 0))])
    return pl.pallas_call(
        functools.partial(_out_ln_kernel, n_in=n_in, alpha=alpha),
        grid=(M // tm,),
        in_specs=in_specs,
        out_specs=pl.BlockSpec((tm, D), lambda i: (i, 0)),
        out_shape=jax.ShapeDtypeStruct((M, D), F32),
        compiler_params=_cparams(("parallel",)),
        name="out_ln",
    )(*ms, *ws, x, g.reshape(1, D), b.reshape(1, D))


def _combine_ln_kernel(x_ref, y0_ref, y1_ref, g_ref, b_ref, o_ref, *, alpha):
    o_ref[...] = _layer_norm(alpha * x_ref[...] + (y0_ref[...] + y1_ref[...]), g_ref[...], b_ref[...])


def combine_ln(x, y0, y1, g, b, alpha, tm):
    M, D = x.shape
    row = pl.BlockSpec((tm, D), lambda i: (i, 0))
    vec = pl.BlockSpec((1, D), lambda i: (0, 0))
    return pl.pallas_call(
        functools.partial(_combine_ln_kernel, alpha=alpha),
        grid=(M // tm,),
        in_specs=[row, row, row, vec, vec],
        out_specs=row,
        out_shape=jax.ShapeDtypeStruct((M, D), F32),
        compiler_params=_cparams(("parallel",)),
        name="combine_ln",
    )(x, y0, y1, g.reshape(1, D), b.reshape(1, D))


def _router_kernel(x_ref, w_ref, rb_ref, e_ref, g_ref):
    logits = _dot_nt(w_ref[...], x_ref[...], precision=HI)
    sc = _sigmoid(logits)
    sel = sc + rb_ref[...]
    G = N_GROUPS
    sels = [sel[l * G:(l + 1) * G] for l in range(EXPERTS_PER_GROUP)]
    scs = [sc[l * G:(l + 1) * G] for l in range(EXPERTS_PER_GROUP)]
    a, b, c, d = sels
    m1, n1, m2, n2 = jnp.maximum(a, b), jnp.minimum(a, b), jnp.maximum(c, d), jnp.minimum(c, d)
    top1 = jnp.maximum(m1, m2)
    top2 = jnp.maximum(jnp.minimum(m1, m2), jnp.maximum(n1, n2))
    gs = top1 + top2
    gi = lax.broadcasted_iota(I32, gs.shape, 0)
    gmax = jnp.max(gs, axis=0, keepdims=True)
    best = jnp.min(jnp.where(gs == gmax, gi, G), axis=0, keepdims=True)
    onehot = gi == best
    v = [jnp.sum(jnp.where(onehot, s, 0.0), axis=0, keepdims=True) for s in sels]
    s = [jnp.sum(jnp.where(onehot, t, 0.0), axis=0, keepdims=True) for t in scs]

    def first_max(vals):
        vmax = functools.reduce(jnp.maximum, vals)
        idx = jnp.full(vmax.shape, len(vals) - 1, I32)
        for l in range(len(vals) - 2, -1, -1):
            idx = jnp.where(vals[l] == vmax, l, idx)
        return idx

    i1 = first_max(v)
    v2 = [jnp.where(i1 == l, -jnp.inf, v[l]) for l in range(EXPERTS_PER_GROUP)]
    i2 = first_max(v2)
    g1 = functools.reduce(lambda acc, l: jnp.where(i1 == l, s[l], acc), range(EXPERTS_PER_GROUP), jnp.zeros_like(s[0]))
    g2 = functools.reduce(lambda acc, l: jnp.where(i2 == l, s[l], acc), range(EXPERTS_PER_GROUP), jnp.zeros_like(s[0]))
    tot = g1 + g2
    e_ref[0:1, :] = best * EXPERTS_PER_GROUP + i1
    e_ref[1:2, :] = best * EXPERTS_PER_GROUP + i2
    g_ref[0:1, :] = g1 / tot
    g_ref[1:2, :] = g2 / tot


def router(x, w_router, router_bias, tm):
    M, D = x.shape
    r = np.arange(N_EXPERTS)
    perm = (r % N_GROUPS) * EXPERTS_PER_GROUP + r // N_GROUPS
    wt = w_router.T[perm]
    rb = router_bias[perm].reshape(N_EXPERTS, 1)
    return pl.pallas_call(
        _router_kernel,
        grid=(M // tm,),
        in_specs=[pl.BlockSpec((tm, D), lambda i: (i, 0)),
                  pl.BlockSpec((N_EXPERTS, D), lambda i: (0, 0)),
                  pl.BlockSpec((N_EXPERTS, 1), lambda i: (0, 0))],
        out_specs=[pl.BlockSpec((TOP_K, tm), lambda i: (0, i)),
                   pl.BlockSpec((TOP_K, tm), lambda i: (0, i))],
        out_shape=[jax.ShapeDtypeStruct((TOP_K, M), I32), jax.ShapeDtypeStruct((TOP_K, M), F32)],
        compiler_params=_cparams(("parallel",)),
        name="router",
    )(x, wt, rb)


def _moe_kernel(be_ref, bflag_ref, bsrc_ref, xs_ref, gs_ref, wg_ref, wu_ref, wd_ref, o_ref, wg_s, wu_s, wd_s):
    del be_ref, bsrc_ref
    i = pl.program_id(0)
    flag = bflag_ref[i]

    @pl.when((flag & 2) != 0)
    def _():
        wg_s[...] = wg_ref[...].astype(BF16)
        wu_s[...] = wu_ref[...].astype(BF16)
        wd_s[...] = wd_ref[...].astype(BF16)

    @pl.when((flag & 1) != 0)
    def _():
        x = xs_ref[...]
        g = _dot(x, wg_s[...])
        u = _dot(x, wu_s[...])
        h = (_silu(g) * u).astype(BF16)
        o_ref[...] = _dot(h, wd_s[...]) * gs_ref[...]


def moe_experts(xs, gs, blk_expert, blk_flag, blk_src, w_gate, w_up, w_down, bm):
    P, D = xs.shape
    DE = w_gate.shape[-1]
    nb = P // bm
    grid_spec = pltpu.PrefetchScalarGridSpec(
        num_scalar_prefetch=3,
        grid=(nb,),
        in_specs=[pl.BlockSpec((bm, D), lambda i, be, bf, bs: (bs[i], 0)),
                  pl.BlockSpec((bm, 1), lambda i, be, bf, bs: (bs[i], 0)),
                  pl.BlockSpec((None, D, DE), lambda i, be, bf, bs: (be[i], 0, 0)),
                  pl.BlockSpec((None, D, DE), lambda i, be, bf, bs: (be[i], 0, 0)),
                  pl.BlockSpec((None, DE, D), lambda i, be, bf, bs: (be[i], 0, 0))],
        out_specs=pl.BlockSpec((bm, D), lambda i, be, bf, bs: (bs[i], 0)),
        scratch_shapes=[pltpu.VMEM((D, DE), BF16), pltpu.VMEM((D, DE), BF16), pltpu.VMEM((DE, D), BF16)],
    )
    return pl.pallas_call(
        _moe_kernel,
        grid_spec=grid_spec,
        out_shape=jax.ShapeDtypeStruct((P, D), F32),
        compiler_params=_cparams(("arbitrary",)),
        name="moe_experts",
    )(blk_expert, blk_flag, blk_src, xs, gs, w_gate, w_up, w_down)


def moe_layer(x, w_router, router_bias, w_gate, w_up, w_down, ln_g, ln_b, alpha, bm, tm):
    M, D = x.shape
    expert, gate = router(x, w_router, router_bias, tm)
    NK = TOP_K * M
    e_flat = expert.reshape(NK)
    g_flat = gate.reshape(NK)
    onehot = (e_flat[:, None] == jnp.arange(N_EXPERTS, dtype=I32)[None, :]).astype(I32)
    csum = jnp.cumsum(onehot, axis=0)
    counts = csum[-1]
    rank = jnp.sum((csum - onehot) * onehot, axis=1)
    padded = (counts + bm - 1) // bm * bm
    pad_end = jnp.cumsum(padded)
    pad_start = pad_end - padded
    dest = pad_start[e_flat] + rank
    nb = -(-NK // bm) + N_EXPERTS
    P = nb * bm
    tok = jnp.arange(NK, dtype=I32) % M
    src_tok = jnp.zeros((P,), I32).at[dest].set(tok, unique_indices=True)
    gs = jnp.zeros((P,), F32).at[dest].set(g_flat, unique_indices=True).reshape(P, 1)
    xs = x.astype(BF16)[src_tok]
    blk_start = jnp.arange(nb, dtype=I32) * bm
    n_valid = pad_end[-1] // bm
    last = jnp.maximum(n_valid - 1, 0)
    blk_src = jnp.minimum(jnp.arange(nb, dtype=I32), last)
    blk_expert_all = jnp.minimum(jnp.searchsorted(pad_end, blk_start, side="right"), N_EXPERTS - 1).astype(I32)
    blk_expert = blk_expert_all[blk_src]
    valid = jnp.arange(nb, dtype=I32) < n_valid
    first = valid & (blk_start == pad_start[blk_expert])
    blk_flag = valid.astype(I32) + 2 * first.astype(I32)
    ys = moe_experts(xs, gs, blk_expert, blk_flag, blk_src.astype(I32), w_gate, w_up, w_down, bm)
    y0 = ys[dest[:M]]
    y1 = ys[dest[M:]]
    return combine_ln(x, y0, y1, ln_g, ln_b, alpha, tm)


def _logf_kernel(fl_ref, b_ref, logf_ref, c_ref, ct_ref, carry_ref, *, tm, seq_len):
    i = pl.program_id(0)
    logf = _log_sigmoid(fl_ref[...] + b_ref[...])
    r = lax.broadcasted_iota(I32, (tm, tm), 0)
    c = lax.broadcasted_iota(I32, (tm, tm), 1)
    if seq_len >= tm:
        tri = (c <= r).astype(F32)
    else:
        tri = ((c <= r) & (r // seq_len == c // seq_len)).astype(F32)
    cs = _dot(tri, logf, precision=HI)
    if seq_len > tm:
        tiles_per_seq = seq_len // tm

        @pl.when(i % tiles_per_seq == 0)
        def _():
            carry_ref[...] = jnp.zeros_like(carry_ref)

        cs = cs + carry_ref[0:1, :]
        carry_ref[...] = jnp.broadcast_to(cs[tm - 1:tm, :], carry_ref.shape)
    logf_ref[...] = logf
    c_ref[...] = cs
    ct_ref[...] = cs.T[:SUBLANES, :]


def logf_cumsum(fl, b_row, row0, nrows, seq_len, tm):
    assert row0 % tm == 0 and nrows % tm == 0 and (seq_len % tm == 0 or tm % seq_len == 0)
    blk0 = row0 // tm
    return pl.pallas_call(
        functools.partial(_logf_kernel, tm=tm, seq_len=seq_len),
        grid=(nrows // tm,),
        in_specs=[pl.BlockSpec((tm, LANES), lambda i: (blk0 + i, 0)),
                  pl.BlockSpec((1, LANES), lambda i: (0, 0))],
        out_specs=[pl.BlockSpec((tm, LANES), lambda i: (i, 0)),
                   pl.BlockSpec((tm, LANES), lambda i: (i, 0)),
                   pl.BlockSpec((SUBLANES, tm), lambda i: (0, i))],
        out_shape=[jax.ShapeDtypeStruct((nrows, LANES), F32),
                   jax.ShapeDtypeStruct((nrows, LANES), F32),
                   jax.ShapeDtypeStruct((SUBLANES, nrows), F32)],
        scratch_shapes=[pltpu.VMEM((SUBLANES, LANES), F32)],
        compiler_params=_cparams(("arbitrary",)),
        name="logf_cumsum",
    )(fl, b_row)


def _suffix_kernel(x_ref, o_ref):
    n = x_ref.shape[1]
    r = lax.broadcasted_iota(I32, (n, n), 0)
    c = lax.broadcasted_iota(I32, (n, n), 1)
    o_ref[...] = _dot(x_ref[...], (r > c).astype(F32), precision=HI)


def suffix_sum(x):
    R, P = x.shape
    return pl.pallas_call(
        _suffix_kernel,
        grid=(1,),
        in_specs=[pl.BlockSpec((R, P), lambda i: (0, 0))],
        out_specs=pl.BlockSpec((R, P), lambda i: (0, 0)),
        out_shape=jax.ShapeDtypeStruct((R, P), F32),
        compiler_params=_cparams(("arbitrary",)),
        name="suffix_sum",
    )(x)


def _pick_lane(x, h):
    lane = lax.broadcasted_iota(I32, x.shape, 1)
    return jnp.sum(jnp.where(lane == h, x, 0.0), axis=1, keepdims=True)


def _fox_prompt_kernel(q_ref, k_ref, v_ref, c_ref, ct_ref, mix_in_ref, o_ref, *, tq):
    del mix_in_ref
    hp = pl.program_id(1)
    i = pl.program_id(2)
    q2 = q_ref[...]
    lane = lax.broadcasted_iota(I32, q2.shape, 1)
    qs = [jnp.where(lane < HEAD_DIM, q2, 0), jnp.where(lane >= HEAD_DIM, q2, 0)]
    cc = c_ref[...]
    ccol = [_pick_lane(cc, 2 * hp), _pick_lane(cc, 2 * hp + 1)]

    def scores(e, j):
        kj = k_ref[pl.ds(pl.multiple_of(j * tq, tq), tq), :].astype(BF16)
        crow = ct_ref[j, pl.ds(2 * hp + e, 1), :]
        return _dot_nt(qs[e], kj) + (ccol[e] - crow)

    def update(e, j, s, m, l, acc):
        vj = v_ref[pl.ds(pl.multiple_of(j * tq, tq), tq), :].astype(BF16)
        m_new = jnp.maximum(m, jnp.max(s, axis=1, keepdims=True))
        alpha = jnp.exp(m - m_new)
        p = jnp.exp(s - m_new)
        l = alpha * l + jnp.sum(p, axis=1, keepdims=True)
        acc = alpha * acc + _dot(p.astype(BF16), vj)
        return m_new, l, acc

    def body(j, carry):
        out = []
        for e in range(2):
            m, l, acc = carry[e]
            out.append(update(e, j, scores(e, j), m, l, acc))
        return tuple(out)

    init = tuple((jnp.full((tq, 1), NEG_INF, F32), jnp.zeros((tq, 1), F32), jnp.zeros((tq, LANES), F32))
                 for _ in range(2))
    carry = lax.fori_loop(0, i, body, init)
    r = lax.broadcasted_iota(I32, (tq, tq), 0)
    c = lax.broadcasted_iota(I32, (tq, tq), 1)
    res = []
    for e in range(2):
        m, l, acc = carry[e]
        s = jnp.where(c <= r, scores(e, i), NEG_INF)
        m, l, acc = update(e, i, s, m, l, acc)
        res.append(acc / l)
    o_ref[...] = jnp.where(lane < HEAD_DIM, res[0], res[1])


def fox_prompt(qf, kf, vf, c, ct, mix, B, T, tq):
    n_hp = qf.shape[1] // LANES
    nq = T // tq
    ct = ct.reshape(SUBLANES, B * nq, tq).transpose(1, 0, 2)
    return pl.pallas_call(
        functools.partial(_fox_prompt_kernel, tq=tq),
        grid=(B, n_hp, nq),
        in_specs=[pl.BlockSpec((tq, LANES), lambda b, h, i: (b * nq + i, h)),
                  pl.BlockSpec((T, LANES), lambda b, h, i: (b, h)),
                  pl.BlockSpec((T, LANES), lambda b, h, i: (b, h)),
                  pl.BlockSpec((tq, LANES), lambda b, h, i: (b * nq + i, 0)),
                  pl.BlockSpec((nq, SUBLANES, tq), lambda b, h, i: (b, 0, 0)),
                  pl.BlockSpec(memory_space=pl.ANY)],
        out_specs=pl.BlockSpec((tq, LANES), lambda b, h, i: (b * nq + i, h)),
        out_shape=jax.ShapeDtypeStruct(mix.shape, mix.dtype),
        input_output_aliases={5: 0},
        compiler_params=_cparams(("parallel", "parallel", "arbitrary")),
        name="fox_prompt",
    )(qf, kf, vf, c, ct, mix)


BAND_TQ = 2 * CHUNK
BAND_WIN = WINDOW_ROWS + BAND_TQ


def _softmax_pv(s_list, v_list):
    m = functools.reduce(jnp.maximum, [jnp.max(s, axis=1, keepdims=True) for s in s_list])
    ps = [jnp.exp(s - m) for s in s_list]
    l = functools.reduce(lambda a, b: a + b, [jnp.sum(p, axis=1, keepdims=True) for p in ps])
    o = functools.reduce(lambda a, b: a + b, [_dot(p.astype(BF16), v) for p, v in zip(ps, v_list)])
    return o / l


def _band_prompt_kernel(q_ref, k_ref, v_ref, bias_ref, mix_in_ref, o_ref):
    del mix_in_ref
    i = pl.program_id(2)
    q2 = q_ref[...]
    lane = lax.broadcasted_iota(I32, q2.shape, 1)
    qs = [jnp.where(lane < HEAD_DIM, q2, 0), jnp.where(lane >= HEAD_DIM, q2, 0)]
    start = pl.multiple_of(i * BAND_TQ, BAND_TQ)
    kw = k_ref[pl.ds(start, BAND_WIN), :].astype(BF16)
    vw = v_ref[pl.ds(start, BAND_WIN), :].astype(BF16)
    col = lax.broadcasted_iota(I32, (BAND_TQ, BAND_WIN), 1)
    valid = col >= WINDOW_ROWS - i * BAND_TQ
    res = []
    for e in range(2):
        s = jnp.where(valid, _dot_nt(qs[e], kw) + bias_ref[e], NEG_INF)
        res.append(_softmax_pv([s], [vw]))
    o_ref[...] = jnp.where(lane < HEAD_DIM, res[0], res[1])


def band_prompt(qb, kpad, vpad, bias, mix, B, T, col0):
    n_hp = qb.shape[1] // LANES
    nq = T // BAND_TQ
    Tp = T + WINDOW_ROWS
    cb0 = col0 // LANES
    return pl.pallas_call(
        _band_prompt_kernel,
        grid=(B, n_hp, nq),
        in_specs=[pl.BlockSpec((BAND_TQ, LANES), lambda b, h, i: (b * nq + i, h)),
                  pl.BlockSpec((Tp, LANES), lambda b, h, i: (b, h)),
                  pl.BlockSpec((Tp, LANES), lambda b, h, i: (b, h)),
                  pl.BlockSpec((2, BAND_TQ, BAND_WIN), lambda b, h, i: (h, 0, 0)),
                  pl.BlockSpec(memory_space=pl.ANY)],
        out_specs=pl.BlockSpec((BAND_TQ, LANES), lambda b, h, i: (b * nq + i, cb0 + h)),
        out_shape=jax.ShapeDtypeStruct(mix.shape, mix.dtype),
        input_output_aliases={4: 0},
        compiler_params=_cparams(("parallel", "parallel", "arbitrary")),
        name="band_prompt",
    )(qb, kpad, vpad, bias, mix)


def _att_sample_kernel(qf_ref, kf_ref, vf_ref, ck_ref, cv_ref, r_ref, cn_ref, cnt_ref,
                       qb_ref, kb_ref, vb_ref, cbk_ref, cbv_ref, bias_ref, mix_in_ref, o_ref, *, n_hp, S, R):
    del mix_in_ref
    lane = lax.broadcasted_iota(I32, (S, LANES), 1)
    r = lax.broadcasted_iota(I32, (S, S), 0)
    c = lax.broadcasted_iota(I32, (S, S), 1)
    cn = cn_ref[...]
    for hp in range(n_hp):
        sl = slice(hp * LANES, (hp + 1) * LANES)
        q2 = qf_ref[:, sl]
        qs = [jnp.where(lane < HEAD_DIM, q2, 0), jnp.where(lane >= HEAD_DIM, q2, 0)]
        kc = ck_ref[:, sl].astype(BF16)
        vc = cv_ref[:, sl].astype(BF16)
        kn = kf_ref[:, sl].astype(BF16)
        vn = vf_ref[:, sl].astype(BF16)
        res = []
        for e in range(2):
            h = 2 * hp + e
            ccol = cn[:, h:h + 1]
            s_c = _dot_nt(qs[e], kc) + (ccol + r_ref[h:h + 1, :])
            s_n = jnp.where(c <= r, _dot_nt(qs[e], kn) + (ccol - cnt_ref[h:h + 1, :]), NEG_INF)
            res.append(_softmax_pv([s_c, s_n], [vc, vn]))
        o_ref[:, sl] = jnp.where(lane < HEAD_DIM, res[0], res[1])
        q2 = qb_ref[:, sl]
        qs = [jnp.where(lane < HEAD_DIM, q2, 0), jnp.where(lane >= HEAD_DIM, q2, 0)]
        kc = cbk_ref[:, sl].astype(BF16)
        vc = cbv_ref[:, sl].astype(BF16)
        kn = kb_ref[:, sl].astype(BF16)
        vn = vb_ref[:, sl].astype(BF16)
        res = []
        for e in range(2):
            h = 2 * hp + e
            s_c = _dot_nt(qs[e], kc) + bias_ref[h, :, 0:R]
            s_n = _dot_nt(qs[e], kn) + bias_ref[h, :, R:R + S]
            res.append(_softmax_pv([s_c, s_n], [vc, vn]))
        osl = slice((n_hp + hp) * LANES, (n_hp + hp + 1) * LANES)
        o_ref[:, osl] = jnp.where(lane < HEAD_DIM, res[0], res[1])


def att_sample(qf, kf, vf, ck, cv, rsuf, cn, cnt, qb, kb, vb, cbk, cbv, bias1, mix, row0, NB, S):
    W = qf.shape[1]
    n_hp = W // LANES
    P = ck.shape[0] // NB
    R = cbk.shape[0] // NB
    H = bias1.shape[0]
    rb0 = row0 // S
    cnt = cnt.reshape(SUBLANES, NB, S).transpose(1, 0, 2)
    new = lambda b: (rb0 + b, 0)
    per_b = lambda b: (b, 0)
    return pl.pallas_call(
        functools.partial(_att_sample_kernel, n_hp=n_hp, S=S, R=R),
        grid=(NB,),
        in_specs=[pl.BlockSpec((S, W), new), pl.BlockSpec((S, W), new), pl.BlockSpec((S, W), new),
                  pl.BlockSpec((P, W), per_b), pl.BlockSpec((P, W), per_b),
                  pl.BlockSpec((SUBLANES, P), per_b),
                  pl.BlockSpec((S, LANES), per_b),
                  pl.BlockSpec((None, SUBLANES, S), lambda b: (b, 0, 0)),
                  pl.BlockSpec((S, W), new), pl.BlockSpec((S, W), new), pl.BlockSpec((S, W), new),
                  pl.BlockSpec((R, W), per_b), pl.BlockSpec((R, W), per_b),
                  pl.BlockSpec((H, S, R + S), lambda b: (0, 0, 0)),
                  pl.BlockSpec(memory_space=pl.ANY)],
        out_specs=pl.BlockSpec((S, 2 * W), new),
        out_shape=jax.ShapeDtypeStruct(mix.shape, mix.dtype),
        input_output_aliases={14: 0},
        compiler_params=_cparams(("parallel",)),
        name="att_sample",
    )(qf, kf, vf, ck, cv, rsuf, cn, cnt, qb, kb, vb, cbk, cbv, bias1, mix)


def _gdn_kernel(qkv_ref, z_ref, ab_ref, cw_ref, alog_ref, dtb_ref, nw_ref, s0_ref, c0_ref, mix_in_ref,
                o_ref, sf_ref, xbuf, s_scr, *, H, DK, L):
    del mix_in_ref
    n = pl.program_id(1)
    n_last = pl.num_programs(1) - 1
    HK = H * DK

    @pl.when(n == 0)
    def _():
        xbuf[0:SUBLANES, :] = c0_ref[...]
        s_scr[...] = s0_ref[...]

    @pl.when(n > 0)
    def _():
        xbuf[0:SUBLANES, :] = xbuf[L:L + SUBLANES, :]

    xbuf[SUBLANES:SUBLANES + L, :] = qkv_ref[...]
    y = None
    for j in range(CONV_W):
        off = SUBLANES - (CONV_W - 1) + j
        t = xbuf[off:off + L, :] * cw_ref[j:j + 1, :]
        y = t if y is None else y + t
    y = _silu(y)

    ab = ab_ref[...]
    g_all = -jnp.exp(alog_ref[...]) * _softplus(ab + dtb_ref[...])
    beta_all = _sigmoid(ab)
    r = lax.broadcasted_iota(I32, (L, L), 0)
    c = lax.broadcasted_iota(I32, (L, L), 1)
    incl = c <= r
    strict = c < r
    eye = c == r
    gcum_all = _dot(incl.astype(F32), g_all, precision=HI)
    eye_f = eye.astype(F32)

    for h in range(H):
        q = y[:, h * DK:(h + 1) * DK]
        k = y[:, HK + h * DK:HK + (h + 1) * DK]
        v = y[:, 2 * HK + h * DK:2 * HK + (h + 1) * DK]
        q = q * lax.rsqrt(jnp.sum(q * q, axis=1, keepdims=True) + NORM_EPS) * (DK ** -0.5)
        k = k * lax.rsqrt(jnp.sum(k * k, axis=1, keepdims=True) + NORM_EPS)
        gcol = gcum_all[:, h:h + 1]
        beta = beta_all[:, H + h:H + h + 1]
        grow = jnp.sum(jnp.where(eye, jnp.broadcast_to(gcol, (L, L)), 0.0), axis=0, keepdims=True)
        diff = gcol - grow
        decay = jnp.exp(jnp.where(incl, diff, NEG_INF))
        kb = k.astype(BF16)
        kk = _dot_nt(kb, kb)
        a_mat = jnp.where(strict, kk * decay, 0.0) * beta
        nmat = -a_mat
        tinv = eye_f + nmat
        for _ in range(5):
            nmat = _mm(nmat, nmat)
            tinv = tinv + _mm(tinv, nmat)
        egc = jnp.exp(gcol)
        rhs = jnp.concatenate([beta * v, (beta * egc) * k], axis=1)
        sol = _mm(tinv, rhs)
        u = sol[:, :DK]
        w = sol[:, DK:]
        qk = _dot_nt(q.astype(BF16), kb) * decay
        glast = gcol[L - 1:L, :]
        k_end = k * jnp.exp(glast - gcol)
        S = s_scr[h]
        Sb = S.astype(BF16)
        delta = u - _dot(w.astype(BF16), Sb)
        o = egc * _dot(q.astype(BF16), Sb) + _dot(qk.astype(BF16), delta.astype(BF16))
        s_scr[h] = jnp.exp(glast) * S + _dot_tn(k_end.astype(BF16), delta.astype(BF16))
        o = o * lax.rsqrt(jnp.mean(o * o, axis=1, keepdims=True) + NORM_EPS) * nw_ref[...]
        o_ref[:, h * DK:(h + 1) * DK] = o * _silu(z_ref[:, h * DK:(h + 1) * DK])

    @pl.when(n == n_last)
    def _():
        sf_ref[...] = s_scr[...]


def gdn(qkv, z, ab, conv_w, a_log_row, dt_bias_row, norm_w, s0, c0, mix, row0, NB, T, H, DK):
    L = GDN_CHUNK
    nc = T // L
    rb0 = row0 // L
    C = qkv.shape[1]
    HK = H * DK
    rows = lambda b, n: (rb0 + b * nc + n, 0)
    const = lambda b, n: (0, 0)
    return pl.pallas_call(
        functools.partial(_gdn_kernel, H=H, DK=DK, L=L),
        grid=(NB, nc),
        in_specs=[pl.BlockSpec((L, C), rows), pl.BlockSpec((L, HK), rows), pl.BlockSpec((L, LANES), rows),
                  pl.BlockSpec((SUBLANES, C), const), pl.BlockSpec((1, LANES), const),
                  pl.BlockSpec((1, LANES), const), pl.BlockSpec((1, DK), const),
                  pl.BlockSpec((None, H, DK, DK), lambda b, n: (b, 0, 0, 0)),
                  pl.BlockSpec((None, SUBLANES, C), lambda b, n: (b, 0, 0)),
                  pl.BlockSpec(memory_space=pl.ANY)],
        out_specs=[pl.BlockSpec((L, HK), rows),
                   pl.BlockSpec((None, H, DK, DK), lambda b, n: (b, 0, 0, 0))],
        out_shape=[jax.ShapeDtypeStruct(mix.shape, mix.dtype),
                   jax.ShapeDtypeStruct((NB, H, DK, DK), F32)],
        input_output_aliases={9: 0},
        scratch_shapes=[pltpu.VMEM((L + 2 * SUBLANES, C), F32), pltpu.VMEM((H, DK, DK), F32)],
        compiler_params=_cparams(("parallel", "arbitrary")),
        name="gdn",
    )(qkv, z, ab, conv_w, a_log_row, dt_bias_row, norm_w, s0, c0, mix)


def _pad_lanes(v, n=LANES):
    return jnp.zeros((1, n), F32).at[0, :v.shape[0]].set(v)


def _band_bias(table, tq):
    rr = np.arange(tq)[:, None]
    jj = np.arange(WINDOW_ROWS + tq)[None, :]
    cr, ii = rr // CHUNK, rr % CHUNK
    kc, jin = jj // CHUNK, jj % CHUNK
    in_band = (kc >= cr) & (kc <= cr + LEFT_CHUNKS)
    krel = (kc - cr - LEFT_CHUNKS) * CHUNK + jin
    idx = np.clip(ii - krel, -REL_CLIP, REL_CLIP) + REL_CLIP
    b = table[:, idx].astype(F32)
    return jnp.where(jnp.asarray(in_band)[None], b, NEG_INF)


def kernel(x_prompt, x_sample, cache_fox_k, cache_fox_v, cache_fox_logf, cache_band_k, cache_band_v, state_gdn, state_gdn_conv, w_in_att, b_forget, rel_bias_table, w_out_att, w_in_gdn, conv_gdn, a_log, dt_bias, gdn_norm_w, w_out_gdn, ln_mix_g, ln_mix_b, ln_ffn_g, ln_ffn_b, w_router, router_bias, w_gate, w_up, w_down):
    B, T, D = x_prompt.shape
    NB, S, _ = x_sample.shape
    depth = ln_mix_g.shape[0]
    P = cache_fox_k.shape[2]
    H_FOX = cache_fox_k.shape[3]
    R = cache_band_k.shape[2]
    H_BAND = cache_band_k.shape[3]
    H_GDN, DK = state_gdn.shape[2], state_gdn.shape[3]
    W_FOX, W_BAND, W_GDN = H_FOX * HEAD_DIM, H_BAND * HEAD_DIM, H_GDN * DK
    CONV_CH = 3 * W_GDN
    MP, MS = B * T, NB * S
    M = MP + MS
    alpha = (2 * depth) ** 0.25
    TM = 256
    scale = HEAD_DIM ** -0.5

    x = jnp.concatenate([x_prompt.reshape(MP, D), x_sample.reshape(MS, D)], axis=0)
    outs = {k: [] for k in ("fkp", "fvp", "flp", "bkp", "bvp", "gsp", "gcp", "fks", "fvs", "fls", "bks", "bvs", "gss", "gcs")}

    for layer in range(depth):
        i = layer // 2
        if layer % 2 == 0:
            w = w_in_att[i]
            cuts = np.cumsum([0, W_FOX, W_FOX, W_FOX, H_FOX, W_BAND, W_BAND, W_BAND])
            wq, wk, wv, wl, wqb, wkb, wvb = [w[:, cuts[j]:cuts[j + 1]] for j in range(7)]
            wl = jnp.pad(wl, ((0, 0), (0, LANES - H_FOX)))
            wcat = jnp.concatenate([wq * scale, wk, wv, wqb * scale, wkb, wvb, wl], axis=1).astype(BF16)
            widths = [W_FOX, W_FOX, W_FOX, W_BAND, W_BAND, W_BAND, LANES]
            starts = np.cumsum([0] + widths[:-1])
            qf, kf, vf, qb, kb, vb, fl = proj(x, wcat, list(zip(starts, widths)),
                                              [BF16, F32, F32, BF16, F32, F32, F32], TM)
            b_row = _pad_lanes(b_forget[i])
            logf_p, c_p, ct_p = logf_cumsum(fl, b_row, 0, MP, T, TM)
            logf_s, c_s, ct_s = logf_cumsum(fl, b_row, MP, MS, S, TM)
            mix = jnp.zeros((M, W_FOX + W_BAND), F32)
            mix = fox_prompt(qf, kf, vf, c_p, ct_p, mix, B, T, TM)
            kpad = jnp.pad(kb[:MP].reshape(B, T, W_BAND), ((0, 0), (WINDOW_ROWS, 0), (0, 0))).reshape(-1, W_BAND)
            vpad = jnp.pad(vb[:MP].reshape(B, T, W_BAND), ((0, 0), (WINDOW_ROWS, 0), (0, 0))).reshape(-1, W_BAND)
            bias2 = _band_bias(rel_bias_table[i], BAND_TQ)
            mix = band_prompt(qb, kpad, vpad, bias2, mix, B, T, W_FOX)
            bias1 = _band_bias(rel_bias_table[i], CHUNK)
            clt = jnp.pad(cache_fox_logf[i].transpose(0, 2, 1), ((0, 0), (0, SUBLANES - H_FOX), (0, 0)))
            rsuf = suffix_sum(clt.reshape(NB * SUBLANES, P))
            mix = att_sample(qf, kf, vf, cache_fox_k[i].reshape(NB * P, W_FOX), cache_fox_v[i].reshape(NB * P, W_FOX),
                             rsuf, c_s, ct_s, qb, kb, vb,
                             cache_band_k[i].reshape(NB * R, W_BAND), cache_band_v[i].reshape(NB * R, W_BAND),
                             bias1, mix, MP, NB, S)
            wo = w_out_att[i].astype(BF16)
            x = out_ln([mix], [wo], x, ln_mix_g[layer], ln_mix_b[layer], alpha, TM)
            outs["fkp"].append(kf[:MP].reshape(B, T, H_FOX, HEAD_DIM))
            outs["fvp"].append(vf[:MP].reshape(B, T, H_FOX, HEAD_DIM))
            outs["flp"].append(logf_p[:, :H_FOX].reshape(B, T, H_FOX))
            outs["bkp"].append(kb[:MP].reshape(B, T, H_BAND, HEAD_DIM)[:, T - R:])
            outs["bvp"].append(vb[:MP].reshape(B, T, H_BAND, HEAD_DIM)[:, T - R:])
            outs["fks"].append(kf[MP:].reshape(NB, S, H_FOX, HEAD_DIM))
            outs["fvs"].append(vf[MP:].reshape(NB, S, H_FOX, HEAD_DIM))
            outs["fls"].append(logf_s[:, :H_FOX].reshape(NB, S, H_FOX))
            kb_s = kb[MP:].reshape(NB, S, H_BAND, HEAD_DIM)
            vb_s = vb[MP:].reshape(NB, S, H_BAND, HEAD_DIM)
            outs["bks"].append(jnp.concatenate([cache_band_k[i], kb_s], axis=1)[:, -R:])
            outs["bvs"].append(jnp.concatenate([cache_band_v[i], vb_s], axis=1)[:, -R:])
        else:
            w = w_in_gdn[i]
            wab = jnp.pad(w[:, CONV_CH + W_GDN:], ((0, 0), (0, LANES - 2 * H_GDN)))
            wcat = jnp.concatenate([w[:, :CONV_CH + W_GDN], wab], axis=1).astype(BF16)
            widths = [CONV_CH, W_GDN, LANES]
            starts = np.cumsum([0] + widths[:-1])
            qkv, z, ab = proj(x, wcat, list(zip(starts, widths)), [F32, F32, F32], TM)
            cw = jnp.pad(conv_gdn[i], ((0, SUBLANES - CONV_W), (0, 0)))
            alog_row = _pad_lanes(a_log[i])
            dtb_row = _pad_lanes(dt_bias[i])
            nw = gdn_norm_w[i].reshape(1, DK)
            mix = jnp.zeros((M, W_GDN), F32)
            mix, s_p = gdn(qkv, z, ab, cw, alog_row, dtb_row, nw,
                           jnp.zeros((B, H_GDN, DK, DK), F32), jnp.zeros((B, SUBLANES, CONV_CH), F32),
                           mix, 0, B, T, H_GDN, DK)
            c0 = jnp.pad(state_gdn_conv[i], ((0, 0), (SUBLANES - (CONV_W - 1), 0), (0, 0)))
            mix, s_s = gdn(qkv, z, ab, cw, alog_row, dtb_row, nw, state_gdn[i], c0, mix, MP, NB, S, H_GDN, DK)
            wo = w_out_gdn[i].astype(BF16)
            x = out_ln([mix], [wo], x, ln_mix_g[layer], ln_mix_b[layer], alpha, TM)
            outs["gsp"].append(s_p)
            outs["gss"].append(s_s)
            outs["gcp"].append(qkv[:MP].reshape(B, T, CONV_CH)[:, T - (CONV_W - 1):])
            outs["gcs"].append(qkv[MP:].reshape(NB, S, CONV_CH)[:, S - (CONV_W - 1):])
        x = moe_layer(x, w_router, router_bias, w_gate[layer], w_up[layer], w_down[layer],
                      ln_ffn_g[layer], ln_ffn_b[layer], alpha, 256, TM)

    st = {k: jnp.stack(v) for k, v in outs.items()}
    return (x[:MP].reshape(B, T, D), x[MP:].reshape(NB, S, D),
            st["fkp"], st["fvp"], st["flp"], st["bkp"], st["bvp"], st["gsp"], st["gcp"],
            st["fks"], st["fvs"], st["fls"], st["bks"], st["bvs"], st["gss"], st["gcs"])
```
